```python
import jax, jax.numpy as jnp
from jax import lax
import numpy as np

D_MODEL = 4096
BATCH = 4
SEQ = 2048
DEPTH = 4
DEC_BATCH = 32
DEC_SEQ = 4
PAST_LEN = 8192
PAGE_SIZE = 128

N_MIXERS = 2
N_ATTN_LAYERS = (DEPTH + 1) // 2
N_LRU_LAYERS = DEPTH // 2
HEAD_DIM = 128
N_HEADS = D_MODEL // HEAD_DIM
N_KV_HEADS = 8
GQA = N_HEADS // N_KV_HEADS
WINDOW = 128
BLOCK = WINDOW
ROT_DIM = HEAD_DIM // 4
ROPE_THETA = 500000.0
LRU_WIDTH = D_MODEL
LRU_BLOCKS = 16
LRU_BLOCK_SIZE = LRU_WIDTH // LRU_BLOCKS
CONV_W = 4
LRU_C = 8.0
N_GROUPS = 8
EXPERTS_PER_GROUP = 8
N_EXPERTS = N_GROUPS * EXPERTS_PER_GROUP
TOP_K = 2
D_FF_EXPERT = D_MODEL // 8
MOE_BLOCK = 128
RMS_EPS = 1e-6

kernel_name = 'hybrid_swa_sink_rglru_hmoe_step'


def rmsnorm(x, g):
    xf = x.astype(jnp.float32)
    y = xf * lax.rsqrt(jnp.mean(xf * xf, axis=-1, keepdims=True) + RMS_EPS)
    return (y * g.astype(jnp.float32)).astype(x.dtype)


def rope(x, pos):
    inv = ROPE_THETA ** (-jnp.arange(0, ROT_DIM, 2, dtype=jnp.float32) / ROT_DIM)
    ang = pos.astype(jnp.float32)[:, None] * inv[None, :]
    cos = jnp.cos(ang)[None, :, None, :]
    sin = jnp.sin(ang)[None, :, None, :]
    xr = x[..., :ROT_DIM].astype(jnp.float32)
    x1, x2 = xr[..., :ROT_DIM // 2], xr[..., ROT_DIM // 2:]
    rot = jnp.concatenate([x1 * cos - x2 * sin, x2 * cos + x1 * sin], axis=-1).astype(x.dtype)
    return jnp.concatenate([rot, x[..., ROT_DIM:]], axis=-1)


def qkv_project(xn, w_qkv):
    B, T, _ = xn.shape
    qkv = xn @ w_qkv
    nq, nk = N_HEADS * HEAD_DIM, N_KV_HEADS * HEAD_DIM
    q = qkv[..., :nq].reshape(B, T, N_HEADS, HEAD_DIM)
    k = qkv[..., nq:nq + nk].reshape(B, T, N_KV_HEADS, HEAD_DIM)
    v = qkv[..., nq + nk:].reshape(B, T, N_KV_HEADS, HEAD_DIM)
    return q, k, v


def sink_attention(q, k, v, mask, sinks):
    s = jnp.einsum('...qkgd,...skd->...kgqs', q, k).astype(jnp.float32) * (HEAD_DIM ** -0.5)
    s = jnp.where(mask, s, -jnp.inf)
    sink = sinks.astype(jnp.float32).reshape(N_KV_HEADS, GQA, 1, 1)
    m = jnp.maximum(jnp.max(s, axis=-1, keepdims=True), sink)
    p = jnp.exp(s - m)
    p = p / (jnp.sum(p, axis=-1, keepdims=True) + jnp.exp(sink - m))
    return jnp.einsum('...kgqs,...skd->...qkgd', p.astype(v.dtype), v)


def attn_prompt(xn, w_qkv, w_o, sinks):
    B, S, _ = xn.shape
    q, k, v = qkv_project(xn, w_qkv)
    pos = jnp.arange(S, dtype=jnp.int32)
    q, k = rope(q, pos), rope(k, pos)
    nb = S // BLOCK
    qb = q.reshape(B, nb, BLOCK, N_KV_HEADS, GQA, HEAD_DIM)
    kb = k.reshape(B, nb, BLOCK, N_KV_HEADS, HEAD_DIM)
    vb = v.reshape(B, nb, BLOCK, N_KV_HEADS, HEAD_DIM)
    kk = jnp.concatenate([jnp.concatenate([jnp.zeros_like(kb[:, :1]), kb[:, :-1]], axis=1), kb], axis=2)
    vv = jnp.concatenate([jnp.concatenate([jnp.zeros_like(vb[:, :1]), vb[:, :-1]], axis=1), vb], axis=2)
    qi = jnp.arange(BLOCK)[:, None]
    si = jnp.arange(2 * BLOCK)[None, :]
    diff = BLOCK + qi - si
    band = (diff >= 0) & (diff < WINDOW)
    valid_key = (jnp.arange(nb)[:, None, None] > 0) | (si[None] >= BLOCK)
    mask = band[None] & valid_key
    o = sink_attention(qb, kk, vv, mask[None, :, None, None], sinks)
    y = o.reshape(B, S, N_HEADS * HEAD_DIM) @ w_o
    rows = min(WINDOW, S)
    return y, k[:, S - rows:], v[:, S - rows:]


def attn_sample(xn, cache_k, cache_v, w_qkv, w_o, sinks):
    B, T, _ = xn.shape
    q, k, v = qkv_project(xn, w_qkv)
    pos = PAST_LEN + jnp.arange(T, dtype=jnp.int32)
    q, k = rope(q, pos), rope(k, pos)
    C = cache_k.shape[1]
    keys = jnp.concatenate([cache_k.astype(k.dtype), k], axis=1)
    vals = jnp.concatenate([cache_v.astype(v.dtype), v], axis=1)
    kpos = jnp.concatenate([PAST_LEN - C + jnp.arange(C, dtype=jnp.int32), pos])
    diff = pos[:, None] - kpos[None, :]
    mask = (diff >= 0) & (diff < WINDOW)
    o = sink_attention(q.reshape(B, T, N_KV_HEADS, GQA, HEAD_DIM), keys, vals, mask, sinks)
    y = o.reshape(B, T, N_HEADS * HEAD_DIM) @ w_o
    return y, keys[:, T:], vals[:, T:]


def rglru_block(xn, h0, conv0, w_in, conv_w, conv_b, w_a, b_a, w_x, b_x, lam, w_out):
    B, T, _ = xn.shape
    proj = xn @ w_in
    gate_br = jax.nn.gelu(proj[..., :LRU_WIDTH])
    u = proj[..., LRU_WIDTH:]
    upad = jnp.concatenate([conv0.astype(u.dtype), u], axis=1)
    c = conv_b + sum(upad[:, t:t + T] * conv_w[t] for t in range(CONV_W))
    cb = c.reshape(B, T, LRU_BLOCKS, LRU_BLOCK_SIZE)
    r = jax.nn.sigmoid((jnp.einsum('btnj,njk->btnk', cb, w_a).reshape(B, T, LRU_WIDTH) + b_a).astype(jnp.float32))
    ig = jax.nn.sigmoid((jnp.einsum('btnj,njk->btnk', cb, w_x).reshape(B, T, LRU_WIDTH) + b_x).astype(jnp.float32))
    log_a = -LRU_C * r * jax.nn.softplus(-lam.astype(jnp.float32))
    a = jnp.exp(log_a)
    b = jnp.sqrt(1.0 - jnp.exp(2.0 * log_a)) * ig * c.astype(jnp.float32)

    def step(h, ab):
        a_t, b_t = ab
        h = a_t * h + b_t
        return h, h

    h_last, hs = lax.scan(step, h0.astype(jnp.float32), (jnp.swapaxes(a, 0, 1), jnp.swapaxes(b, 0, 1)))
    hs = jnp.swapaxes(hs, 0, 1).astype(xn.dtype)
    y = (hs * gate_br) @ w_out
    return y, h_last.astype(xn.dtype), upad[:, -(CONV_W - 1):]


def hier_moe(x, w_rg, b_rg, w_re, b_re, w1, w3, w2):
    B, T0, D = x.shape
    x2 = x.reshape(B * T0, D)
    T = B * T0
    g_prob = jax.nn.softmax((x2 @ w_rg).astype(jnp.float32) + b_rg.astype(jnp.float32), axis=-1)
    g_idx = jnp.argmax(g_prob, axis=-1).astype(jnp.int32)
    g_p = jnp.take_along_axis(g_prob, g_idx[:, None], axis=-1)[:, 0]
    e_log = ((x2 @ w_re).astype(jnp.float32) + b_re.astype(jnp.float32)).reshape(T, N_GROUPS, EXPERTS_PER_GROUP)
    e_log = jnp.take_along_axis(e_log, g_idx[:, None, None], axis=1)[:, 0]
    top_p, top_i = lax.top_k(jax.nn.softmax(e_log, axis=-1), TOP_K)
    gate = g_p[:, None] * top_p / jnp.sum(top_p, axis=-1, keepdims=True)
    expert = g_idx[:, None] * EXPERTS_PER_GROUP + top_i.astype(jnp.int32)
    A = T * TOP_K
    flat_e = expert.reshape(A)
    flat_tok = jnp.repeat(jnp.arange(T, dtype=jnp.int32), TOP_K)
    flat_g = gate.reshape(A)
    order = jnp.argsort(flat_e, stable=True)
    se, stok, sg = flat_e[order], flat_tok[order], flat_g[order]
    counts = jnp.bincount(flat_e, length=N_EXPERTS)
    pcounts = (counts + MOE_BLOCK - 1) // MOE_BLOCK * MOE_BLOCK
    start = jnp.cumsum(counts) - counts
    pend = jnp.cumsum(pcounts)
    pstart = pend - pcounts
    dest = pstart[se] + (jnp.arange(A, dtype=jnp.int32) - start[se])
    nblk = -(-A // MOE_BLOCK) + N_EXPERTS
    P = nblk * MOE_BLOCK
    buf_tok = jnp.full((P,), T, dtype=jnp.int32).at[dest].set(stok)
    x_pad = jnp.concatenate([x2, jnp.zeros((1, D), x2.dtype)], axis=0)
    xb = x_pad[buf_tok].reshape(nblk, MOE_BLOCK, D)
    blk_e = jnp.minimum(jnp.searchsorted(pend, jnp.arange(nblk, dtype=jnp.int32) * MOE_BLOCK, side='right'), N_EXPERTS - 1)

    def expert_block(args):
        xblk, e = args
        hdn = jax.nn.silu(xblk @ w1[e]) * (xblk @ w3[e])
        return hdn @ w2[e]

    yb = lax.map(expert_block, (xb, blk_e)).reshape(P, D)
    contrib = yb[dest].astype(jnp.float32) * sg[:, None]
    out = jax.ops.segment_sum(contrib, stok, num_segments=T)
    return out.astype(x.dtype).reshape(B, T0, D)


def setup_inputs(seed: int = 0) -> dict:
    key = jax.random.key(seed)
    keys = iter(jax.random.split(key, 40))

    def nrm(shape, scale):
        return jax.random.normal(next(keys), shape, jnp.float32) * scale

    C = min(WINDOW, PAST_LEN)
    a0 = jax.random.uniform(next(keys), (N_LRU_LAYERS, LRU_WIDTH), jnp.float32, 0.9, 0.999)
    return {
        'x_prompt': nrm((BATCH, SEQ, D_MODEL), 1.0),
        'x_sample': nrm((DEC_BATCH, DEC_SEQ, D_MODEL), 1.0),
        'cache_k': nrm((N_ATTN_LAYERS, DEC_BATCH, C, N_KV_HEADS, HEAD_DIM), 1.0),
        'cache_v': nrm((N_ATTN_LAYERS, DEC_BATCH, C, N_KV_HEADS, HEAD_DIM), 1.0),
        'state_h': nrm((N_LRU_LAYERS, DEC_BATCH, LRU_WIDTH), 0.5),
        'state_conv': nrm((N_LRU_LAYERS, DEC_BATCH, CONV_W - 1, LRU_WIDTH), 1.0),
        'norm_mix': 1.0 + nrm((DEPTH, D_MODEL), 0.05),
        'norm_ffn': 1.0 + nrm((DEPTH, D_MODEL), 0.05),
        'norm_final': 1.0 + nrm((D_MODEL,), 0.05),
        'attn_w_qkv': nrm((N_ATTN_LAYERS, D_MODEL, (N_HEADS + 2 * N_KV_HEADS) * HEAD_DIM), D_MODEL ** -0.5),
        'attn_w_o': nrm((N_ATTN_LAYERS, N_HEADS * HEAD_DIM, D_MODEL), (N_HEADS * HEAD_DIM) ** -0.5),
        'attn_sinks': nrm((N_ATTN_LAYERS, N_HEADS), 1.0),
        'lru_w_in': nrm((N_LRU_LAYERS, D_MODEL, 2 * LRU_WIDTH), D_MODEL ** -0.5),
        'lru_conv_w': nrm((N_LRU_LAYERS, CONV_W, LRU_WIDTH), CONV_W ** -0.5),
        'lru_conv_b': nrm((N_LRU_LAYERS, LRU_WIDTH), 0.02),
        'lru_w_a': nrm((N_LRU_LAYERS, LRU_BLOCKS, LRU_BLOCK_SIZE, LRU_BLOCK_SIZE), LRU_BLOCK_SIZE ** -0.5),
        'lru_b_a': nrm((N_LRU_LAYERS, LRU_WIDTH), 0.02),
        'lru_w_x': nrm((N_LRU_LAYERS, LRU_BLOCKS, LRU_BLOCK_SIZE, LRU_BLOCK_SIZE), LRU_BLOCK_SIZE ** -0.5),
        'lru_b_x': nrm((N_LRU_LAYERS, LRU_WIDTH), 0.02),
        'lru_lambda': jnp.log(a0) - jnp.log1p(-a0),
        'lru_w_out': nrm((N_LRU_LAYERS, LRU_WIDTH, D_MODEL), LRU_WIDTH ** -0.5),
        'moe_w_router_group': nrm((DEPTH, D_MODEL, N_GROUPS), D_MODEL ** -0.5),
        'moe_b_router_group': nrm((DEPTH, N_GROUPS), 0.01),
        'moe_w_router_expert': nrm((DEPTH, D_MODEL, N_EXPERTS), D_MODEL ** -0.5),
        'moe_b_router_expert': nrm((DEPTH, N_EXPERTS), 0.01),
        'moe_w1': nrm((DEPTH, N_EXPERTS, D_MODEL, D_FF_EXPERT), D_MODEL ** -0.5),
        'moe_w3': nrm((DEPTH, N_EXPERTS, D_MODEL, D_FF_EXPERT), D_MODEL ** -0.5),
        'moe_w2': nrm((DEPTH, N_EXPERTS, D_FF_EXPERT, D_MODEL), D_FF_EXPERT ** -0.5),
    }


def reference(x_prompt, x_sample, cache_k, cache_v, state_h, state_conv,
              norm_mix, norm_ffn, norm_final,
              attn_w_qkv, attn_w_o, attn_sinks,
              lru_w_in, lru_conv_w, lru_conv_b, lru_w_a, lru_b_a, lru_w_x, lru_b_x, lru_lambda, lru_w_out,
              moe_w_router_group, moe_b_router_group, moe_w_router_expert, moe_b_router_expert,
              moe_w1, moe_w3, moe_w2):
    xp, xs = x_prompt, x_sample
    Bp = xp.shape[0]
    kp_l, vp_l, hp_l, cp_l = [], [], [], []
    ks_l, vs_l, hs_l, cs_l = [], [], [], []
    for i in range(DEPTH):
        j = i // N_MIXERS
        hpn, hsn = rmsnorm(xp, norm_mix[i]), rmsnorm(xs, norm_mix[i])
        if i % N_MIXERS == 0:
            yp, kp, vp = attn_prompt(hpn, attn_w_qkv[j], attn_w_o[j], attn_sinks[j])
            ys, kn, vn = attn_sample(hsn, cache_k[j], cache_v[j], attn_w_qkv[j], attn_w_o[j], attn_sinks[j])
            kp_l.append(kp); vp_l.append(vp); ks_l.append(kn); vs_l.append(vn)
        else:
            lru_args = (lru_w_in[j], lru_conv_w[j], lru_conv_b[j], lru_w_a[j], lru_b_a[j],
                        lru_w_x[j], lru_b_x[j], lru_lambda[j], lru_w_out[j])
            h0 = jnp.zeros((Bp, LRU_WIDTH), xp.dtype)
            c0 = jnp.zeros((Bp, CONV_W - 1, LRU_WIDTH), xp.dtype)
            yp, hp, cp = rglru_block(hpn, h0, c0, *lru_args)
            ys, hn, cn = rglru_block(hsn, state_h[j], state_conv[j], *lru_args)
            hp_l.append(hp); cp_l.append(cp); hs_l.append(hn); cs_l.append(cn)
        xp = xp + yp
        xs = xs + ys
        moe_args = (moe_w_router_group[i], moe_b_router_group[i], moe_w_router_expert[i],
                    moe_b_router_expert[i], moe_w1[i], moe_w3[i], moe_w2[i])
        xp = xp + hier_moe(rmsnorm(xp, norm_ffn[i]), *moe_args)
        xs = xs + hier_moe(rmsnorm(xs, norm_ffn[i]), *moe_args)
    y_prompt = rmsnorm(xp, norm_final)
    y_sample = rmsnorm(xs, norm_final)
    return (y_prompt, y_sample,
            jnp.stack(kp_l), jnp.stack(vp_l), jnp.stack(hp_l), jnp.stack(cp_l),
            jnp.stack(ks_l), jnp.stack(vs_l), jnp.stack(hs_l), jnp.stack(cs_l))
```

```python
import functools

import jax
import jax.numpy as jnp
from jax import lax
from jax.experimental import pallas as pl
from jax.experimental.pallas import tpu as pltpu

F32 = jnp.float32
BF16 = jnp.bfloat16

HEAD_DIM = 128
N_KV_HEADS = 8
GQA = 4
N_HEADS = N_KV_HEADS * GQA
WINDOW = 128
ROT_DIM = HEAD_DIM // 4
ROPE_THETA = 500000.0
PAST_LEN = 8192
LRU_BLOCK_SIZE = 256
CONV_W = 4
LRU_C = 8.0
N_GROUPS = 8
EXPERTS_PER_GROUP = 8
N_EXPERTS = N_GROUPS * EXPERTS_PER_GROUP
RMS_EPS = 1e-6

LANES = 128
SUBLANES = 8
VMEM_LIMIT = 56 * 1024 * 1024

MM_TM = 640
MM_TN = 512
NORM_TM = 320
MOE_ROWS = 128
LRU_TC = 512
LRU_WT = 512
ROUTE_LANES = 128


def _params(sem, vmem=VMEM_LIMIT):
    return pltpu.CompilerParams(dimension_semantics=sem, vmem_limit_bytes=vmem)


def _rms(x, g):
    ms = jnp.mean(x * x, axis=-1, keepdims=True)
    return x * lax.rsqrt(ms + RMS_EPS) * g


def _norm_kernel(x_ref, g_ref, o_ref):
    o_ref[...] = _rms(x_ref[...], g_ref[...]).astype(o_ref.dtype)


def rmsnorm(x, g, out_dtype):
    m, d = x.shape
    return pl.pallas_call(
        _norm_kernel,
        grid=(m // NORM_TM,),
        in_specs=[pl.BlockSpec((NORM_TM, d), lambda i: (i, 0)),
                  pl.BlockSpec((1, d), lambda i: (0, 0))],
        out_specs=pl.BlockSpec((NORM_TM, d), lambda i: (i, 0)),
        out_shape=jax.ShapeDtypeStruct((m, d), out_dtype),
        compiler_params=_params(("arbitrary",)),
        name="rmsnorm",
    )(x, g.reshape(1, d))


def _first_argmax(vals, lane, big):
    mx = jnp.max(vals, axis=-1, keepdims=True)
    idx = jnp.min(jnp.where(vals == mx, lane, big), axis=-1, keepdims=True)
    return mx, idx


def _norm_router_kernel(x_ref, g_ref, w_ref, b_ref, xn_ref, r_ref):
    xn = _rms(x_ref[...], g_ref[...])
    xn_ref[...] = xn
    logits = jnp.dot(xn.astype(BF16), w_ref[...].astype(BF16), preferred_element_type=F32) + b_ref[...]
    tm = logits.shape[0]
    lane = lax.broadcasted_iota(jnp.int32, (tm, ROUTE_LANES), 1)
    neg = jnp.float32(-jnp.inf)
    gl = jnp.where(lane < N_GROUPS, logits, neg)
    gmax = jnp.max(gl, axis=-1, keepdims=True)
    gexp = jnp.exp(gl - gmax)
    gprob = gexp / jnp.sum(gexp, axis=-1, keepdims=True)
    g_p, g_idx = _first_argmax(gprob, lane, ROUTE_LANES)
    lo = N_GROUPS + g_idx * EXPERTS_PER_GROUP
    el = jnp.where((lane >= lo) & (lane < lo + EXPERTS_PER_GROUP), logits, neg)
    emax = jnp.max(el, axis=-1, keepdims=True)
    eexp = jnp.exp(el - emax)
    eprob = eexp / jnp.sum(eexp, axis=-1, keepdims=True)
    p1, i1 = _first_argmax(eprob, lane, ROUTE_LANES)
    in_group = (lane >= lo) & (lane < lo + EXPERTS_PER_GROUP)
    rest = jnp.where(in_group & (lane != i1), eprob, -1.0)
    p2, i2 = _first_argmax(rest, lane, ROUTE_LANES)
    psum = p1 + p2
    gate1 = g_p * p1 / psum
    gate2 = g_p * p2 / psum
    e1 = (i1 - N_GROUPS).astype(F32)
    e2 = (i2 - N_GROUPS).astype(F32)
    rec = jnp.where(lane == 0, gate1, 0.0)
    rec = jnp.where(lane == 1, gate2, rec)
    rec = jnp.where(lane == 2, e1, rec)
    rec = jnp.where(lane == 3, e2, rec)
    r_ref[...] = rec


def norm_router(x, g, w_router, b_router):
    m, d = x.shape
    return pl.pallas_call(
        _norm_router_kernel,
        grid=(m // NORM_TM,),
        in_specs=[pl.BlockSpec((NORM_TM, d), lambda i: (i, 0)),
                  pl.BlockSpec((1, d), lambda i: (0, 0)),
                  pl.BlockSpec((d, ROUTE_LANES), lambda i: (0, 0)),
                  pl.BlockSpec((1, ROUTE_LANES), lambda i: (0, 0))],
        out_specs=[pl.BlockSpec((NORM_TM, d), lambda i: (i, 0)),
                   pl.BlockSpec((NORM_TM, ROUTE_LANES), lambda i: (i, 0))],
        out_shape=[jax.ShapeDtypeStruct((m, d), F32),
                   jax.ShapeDtypeStruct((m, ROUTE_LANES), F32)],
        compiler_params=_params(("arbitrary",)),
        name="norm_router",
    )(x, g.reshape(1, d), w_router, b_router)


def _matmul_kernel(x_ref, w_ref, *rest, has_res):
    if has_res:
        r_ref, o_ref, wb_ref = rest
    else:
        o_ref, wb_ref = rest

    @pl.when(pl.program_id(1) == 0)
    def _():
        wb_ref[...] = w_ref[...].astype(BF16)

    acc = jnp.dot(x_ref[...], wb_ref[...], preferred_element_type=F32)
    if has_res:
        acc = r_ref[...] + acc
    o_ref[...] = acc


def matmul(x, w, res=None):
    m, k = x.shape
    n = w.shape[1]
    in_specs = [pl.BlockSpec((MM_TM, k), lambda j, i: (i, 0)),
                pl.BlockSpec((k, MM_TN), lambda j, i: (0, j))]
    args = [x, w]
    if res is not None:
        in_specs.append(pl.BlockSpec((MM_TM, MM_TN), lambda j, i: (i, j)))
        args.append(res)
    return pl.pallas_call(
        functools.partial(_matmul_kernel, has_res=res is not None),
        grid=(n // MM_TN, m // MM_TM),
        in_specs=in_specs,
        out_specs=pl.BlockSpec((MM_TM, MM_TN), lambda j, i: (i, j)),
        out_shape=jax.ShapeDtypeStruct((m, n), F32),
        scratch_shapes=[pltpu.VMEM((k, MM_TN), BF16)],
        compiler_params=_params(("arbitrary", "arbitrary")),
        name="matmul_res" if res is not None else "matmul",
    )(*args)


def _rope_tables(pos):
    half = ROT_DIM // 2
    inv = ROPE_THETA ** (-jnp.arange(0, ROT_DIM, 2, dtype=F32) / ROT_DIM)
    ang = pos.astype(F32)[:, None] * inv[None, :]
    cos, sin = jnp.cos(ang), jnp.sin(ang)
    t = pos.shape[0]
    zeros = lambda n: jnp.zeros((t, n), F32)
    c = jnp.concatenate([cos, cos, jnp.ones((t, HEAD_DIM - ROT_DIM), F32)], axis=1)
    s1 = jnp.concatenate([-sin, zeros(HEAD_DIM - half)], axis=1)
    s2 = jnp.concatenate([zeros(half), sin, zeros(HEAD_DIM - ROT_DIM)], axis=1)
    return c, s1, s2


def _rope(x, c, s1, s2):
    n = x.shape[-1]
    half = ROT_DIM // 2
    return x * c + pltpu.roll(x, n - half, 1) * s1 + pltpu.roll(x, half, 1) * s2


def _attn_prompt_kernel(sink_ref, q_ref, kv_ref, c_ref, s1_ref, s2_ref,
                        o_ref, ko_ref, vo_ref, kprev, vprev):
    j = pl.program_id(1)
    blk = WINDOW
    kvw = N_KV_HEADS * HEAD_DIM

    @pl.when(j == 0)
    def _():
        kprev[...] = jnp.zeros_like(kprev)
        vprev[...] = jnp.zeros_like(vprev)

    c, s1, s2 = c_ref[...], s1_ref[...], s2_ref[...]
    row = lax.broadcasted_iota(jnp.int32, (GQA * blk, 2 * blk), 0) & (blk - 1)
    col = lax.broadcasted_iota(jnp.int32, (GQA * blk, 2 * blk), 1)
    diff = blk + row - col
    valid = (diff >= 0) & (diff < WINDOW) & ((col >= blk) | (j > 0))
    scale = HEAD_DIM ** -0.5
    for h in range(N_KV_HEADS):
        hs = slice(h * HEAD_DIM, (h + 1) * HEAD_DIM)
        k_cur = _rope(kv_ref[:, hs], c, s1, s2)
        v_cur = kv_ref[:, kvw + h * HEAD_DIM: kvw + (h + 1) * HEAD_DIM]
        ko_ref[0, :, hs] = k_cur
        vo_ref[0, :, hs] = v_cur
        k_cur_b = k_cur.astype(BF16)
        v_cur_b = v_cur.astype(BF16)
        kb = jnp.concatenate([kprev[:, hs], k_cur_b], axis=0)
        vb = jnp.concatenate([vprev[:, hs], v_cur_b], axis=0)
        qs = jnp.concatenate(
            [_rope(q_ref[:, (h * GQA + g) * HEAD_DIM:(h * GQA + g + 1) * HEAD_DIM], c, s1, s2)
             for g in range(GQA)], axis=0).astype(BF16)
        s = lax.dot_general(qs, kb, (((1,), (1,)), ((), ())), preferred_element_type=F32) * scale
        s = jnp.where(valid, s, -jnp.inf)
        ps = []
        for g in range(GQA):
            sg = s[g * blk:(g + 1) * blk]
            sink = sink_ref[h * GQA + g]
            m = jnp.maximum(jnp.max(sg, axis=-1, keepdims=True), sink)
            p = jnp.exp(sg - m)
            p = p / (jnp.sum(p, axis=-1, keepdims=True) + jnp.exp(sink - m))
            ps.append(p.astype(BF16))
        o = jnp.dot(jnp.concatenate(ps, axis=0), vb, preferred_element_type=F32)
        for g in range(GQA):
            o_ref[:, (h * GQA + g) * HEAD_DIM:(h * GQA + g + 1) * HEAD_DIM] = (
                o[g * blk:(g + 1) * blk].astype(o_ref.dtype))
        kprev[:, hs] = k_cur_b
        vprev[:, hs] = v_cur_b


def attn_prompt(qkv, sinks, batch, seq):
    nb = seq // WINDOW
    qw = N_HEADS * HEAD_DIM
    kvw = N_KV_HEADS * HEAD_DIM
    tabs = _rope_tables(jnp.arange(seq, dtype=jnp.int32))
    tab_spec = pl.BlockSpec((WINDOW, HEAD_DIM), lambda b, j: (j, 0))
    return pl.pallas_call(
        _attn_prompt_kernel,
        grid=(batch, nb),
        in_specs=[pl.BlockSpec(memory_space=pltpu.SMEM),
                  pl.BlockSpec((WINDOW, qw), lambda b, j: (b * nb + j, 0)),
                  pl.BlockSpec((WINDOW, 2 * kvw), lambda b, j: (b * nb + j, qw // (2 * kvw))),
                  tab_spec, tab_spec, tab_spec],
        out_specs=[pl.BlockSpec((WINDOW, qw), lambda b, j: (b * nb + j, 0)),
                   pl.BlockSpec((1, WINDOW, kvw), lambda b, j: (b, 0, 0)),
                   pl.BlockSpec((1, WINDOW, kvw), lambda b, j: (b, 0, 0))],
        out_shape=[jax.ShapeDtypeStruct((batch * seq, qw), BF16),
                   jax.ShapeDtypeStruct((batch, WINDOW, kvw), F32),
                   jax.ShapeDtypeStruct((batch, WINDOW, kvw), F32)],
        scratch_shapes=[pltpu.VMEM((WINDOW, kvw), BF16), pltpu.VMEM((WINDOW, kvw), BF16)],
        compiler_params=_params(("arbitrary", "arbitrary")),
        name="attn_prompt",
    )(sinks, qkv, qkv, *tabs)


def _attn_sample_kernel(sink_ref, q_ref, kn_ref, vn_ref, ck_ref, cv_ref,
                        cq_ref, s1q_ref, s2q_ref, ck_t_ref, s1k_ref, s2k_ref,
                        o_ref, nk_ref, nv_ref, kcat, vcat, *, t_new):
    cache = ck_ref.shape[1]
    rows = GQA * t_new

    @pl.when(pl.program_id(0) == 0)
    def _():
        kcat[...] = jnp.zeros_like(kcat)
        vcat[...] = jnp.zeros_like(vcat)

    k_new = _rope(kn_ref[0], ck_t_ref[...], s1k_ref[...], s2k_ref[...])
    v_new = vn_ref[0]
    kcat[0:cache, :] = ck_ref[0]
    vcat[0:cache, :] = cv_ref[0]
    kcat[cache:cache + t_new, :] = k_new
    vcat[cache:cache + t_new, :] = v_new
    nk_ref[0] = kcat[t_new:t_new + cache, :]
    nv_ref[0] = vcat[t_new:t_new + cache, :]

    ncol = kcat.shape[0]
    row = lax.broadcasted_iota(jnp.int32, (rows, ncol), 0)
    col = lax.broadcasted_iota(jnp.int32, (rows, ncol), 1)
    grp = sum((row >= g * t_new).astype(jnp.int32) for g in range(1, GQA))
    t = row - grp * t_new
    valid = ((col < cache) & (col > t)) | ((col >= cache) & (col - cache <= t))
    scale = HEAD_DIM ** -0.5
    cq, s1q, s2q = cq_ref[...], s1q_ref[...], s2q_ref[...]
    for h in range(N_KV_HEADS):
        hs = slice(h * HEAD_DIM, (h + 1) * HEAD_DIM)
        q = _rope(q_ref[0, h], cq, s1q, s2q).astype(BF16)
        kb = kcat[:, hs].astype(BF16)
        vb = vcat[:, hs].astype(BF16)
        s = lax.dot_general(q, kb, (((1,), (1,)), ((), ())), preferred_element_type=F32) * scale
        s = jnp.where(valid, s, -jnp.inf)
        sink = jnp.zeros((rows, 1), F32)
        for g in range(GQA):
            sink = jnp.where(grp[:, 0:1] == g, sink_ref[h * GQA + g], sink)
        m = jnp.maximum(jnp.max(s, axis=-1, keepdims=True), sink)
        p = jnp.exp(s - m)
        p = p / (jnp.sum(p, axis=-1, keepdims=True) + jnp.exp(sink - m))
        o_ref[0, h] = jnp.dot(p.astype(BF16), vb, preferred_element_type=F32)


def attn_sample(q_arr, k_new, v_new, cache_k, cache_v, sinks):
    b, _, rows, _ = q_arr.shape
    t_new = k_new.shape[1]
    cache = cache_k.shape[1]
    kvw = N_KV_HEADS * HEAD_DIM
    pos = PAST_LEN + jnp.arange(t_new, dtype=jnp.int32)
    tq = _rope_tables(jnp.tile(pos, GQA))
    tk = tuple(jnp.tile(a, (1, N_KV_HEADS)) for a in _rope_tables(pos))
    full = lambda shape: pl.BlockSpec(shape, lambda i: (0,) * len(shape))
    return pl.pallas_call(
        functools.partial(_attn_sample_kernel, t_new=t_new),
        grid=(b,),
        in_specs=[pl.BlockSpec(memory_space=pltpu.SMEM),
                  pl.BlockSpec((1, N_KV_HEADS, rows, HEAD_DIM), lambda i: (i, 0, 0, 0)),
                  pl.BlockSpec((1, t_new, kvw), lambda i: (i, 0, 0)),
                  pl.BlockSpec((1, t_new, kvw), lambda i: (i, 0, 0)),
                  pl.BlockSpec((1, cache, kvw), lambda i: (i, 0, 0)),
                  pl.BlockSpec((1, cache, kvw), lambda i: (i, 0, 0)),
                  full((rows, HEAD_DIM)), full((rows, HEAD_DIM)), full((rows, HEAD_DIM)),
                  full((t_new, kvw)), full((t_new, kvw)), full((t_new, kvw))],
        out_specs=[pl.BlockSpec((1, N_KV_HEADS, rows, HEAD_DIM), lambda i: (i, 0, 0, 0)),
                   pl.BlockSpec((1, cache, kvw), lambda i: (i, 0, 0)),
                   pl.BlockSpec((1, cache, kvw), lambda i: (i, 0, 0))],
        out_shape=[jax.ShapeDtypeStruct((b, N_KV_HEADS, rows, HEAD_DIM), F32),
                   jax.ShapeDtypeStruct((b, cache, kvw), F32),
                   jax.ShapeDtypeStruct((b, cache, kvw), F32)],
        scratch_shapes=[pltpu.VMEM((2 * cache, kvw), F32), pltpu.VMEM((2 * cache, kvw), F32)],
        compiler_params=_params(("arbitrary",)),
        name="attn_sample",
    )(sinks, q_arr, k_new, v_new, cache_k, cache_v, *tq, *tk)


def _lru_gates(conv, wa_ref, wx_ref, ba, bx, lam):
    nblk = conv.shape[1] // LRU_BLOCK_SIZE
    ra, rx = [], []
    for n in range(nblk):
        cb = conv[:, n * LRU_BLOCK_SIZE:(n + 1) * LRU_BLOCK_SIZE].astype(BF16)
        ra.append(jnp.dot(cb, wa_ref[n].astype(BF16), preferred_element_type=F32))
        rx.append(jnp.dot(cb, wx_ref[n].astype(BF16), preferred_element_type=F32))
    r = jax.nn.sigmoid(jnp.concatenate(ra, axis=1) + ba)
    ig = jax.nn.sigmoid(jnp.concatenate(rx, axis=1) + bx)
    log_a = -LRU_C * r * jax.nn.softplus(-lam)
    a = jnp.exp(log_a)
    b = jnp.sqrt(1.0 - jnp.exp(2.0 * log_a)) * ig * conv
    return a, b


def _lru_prompt_kernel(gate_ref, u_ref, cw_ref, cb_ref, wa_ref, wx_ref, ba_ref, bx_ref, lam_ref,
                       o_ref, h_ref, ct_ref, ubuf, a_scr, b_scr, h_scr):
    tc = u_ref.shape[0]
    tail = CONV_W - 1
    base = SUBLANES

    @pl.when(pl.program_id(2) == 0)
    def _():
        ubuf[0:base, :] = jnp.zeros((base, ubuf.shape[1]), F32)
        h_scr[...] = jnp.zeros_like(h_scr)

    u = u_ref[...]
    ubuf[base:base + tc, :] = u
    cw = cw_ref[...]
    taps = 0.0
    for t in range(CONV_W):
        off = base - tail + t
        taps = taps + ubuf[off:off + tc, :] * cw[t:t + 1, :]
    conv = cb_ref[...] + taps
    ct_ref[0] = ubuf[base + tc - tail:base + tc, :]
    ubuf[base - tail:base, :] = ubuf[base + tc - tail:base + tc, :]

    a, b = _lru_gates(conv, wa_ref, wx_ref, ba_ref[...], bx_ref[...], lam_ref[...])

    sub = lax.broadcasted_iota(jnp.int32, a.shape, 0) & (SUBLANES - 1)
    for d in (1, 2, 4):
        keep = sub >= d
        a_sh = jnp.where(keep, pltpu.roll(a, d, 0), 1.0)
        b_sh = jnp.where(keep, pltpu.roll(b, d, 0), 0.0)
        b = a * b_sh + b
        a = a * a_sh
    a_scr[...] = a
    b_scr[...] = b

    def group(gi, h):
        r0 = pl.multiple_of(gi * SUBLANES, SUBLANES)
        hg = a_scr[pl.ds(r0, SUBLANES), :] * h + b_scr[pl.ds(r0, SUBLANES), :]
        b_scr[pl.ds(r0, SUBLANES), :] = hg
        return hg[SUBLANES - 1:SUBLANES, :]

    h_last = lax.fori_loop(0, tc // SUBLANES, group, h_scr[...])
    h_scr[...] = h_last
    h_ref[0] = h_last
    o_ref[...] = (b_scr[...] * jax.nn.gelu(gate_ref[...])).astype(o_ref.dtype)


def _lru_specs(wt):
    vec = (lambda i, *_: (0, i))
    return [pl.BlockSpec((CONV_W, wt), vec), pl.BlockSpec((1, wt), vec),
            pl.BlockSpec((wt // LRU_BLOCK_SIZE, LRU_BLOCK_SIZE, LRU_BLOCK_SIZE), lambda i, *_: (i, 0, 0)),
            pl.BlockSpec((wt // LRU_BLOCK_SIZE, LRU_BLOCK_SIZE, LRU_BLOCK_SIZE), lambda i, *_: (i, 0, 0)),
            pl.BlockSpec((1, wt), vec), pl.BlockSpec((1, wt), vec), pl.BlockSpec((1, wt), vec)]


def lru_prompt(proj, batch, seq, p):
    w = proj.shape[1] // 2
    wt, tc = LRU_WT, LRU_TC
    nc = seq // tc
    nw = w // wt
    return pl.pallas_call(
        _lru_prompt_kernel,
        grid=(nw, batch, nc),
        in_specs=[pl.BlockSpec((tc, wt), lambda i, b, c: (b * nc + c, i)),
                  pl.BlockSpec((tc, wt), lambda i, b, c: (b * nc + c, nw + i))] + _lru_specs(wt),
        out_specs=[pl.BlockSpec((tc, wt), lambda i, b, c: (b * nc + c, i)),
                   pl.BlockSpec((1, 1, wt), lambda i, b, c: (b, 0, i)),
                   pl.BlockSpec((1, CONV_W - 1, wt), lambda i, b, c: (b, 0, i))],
        out_shape=[jax.ShapeDtypeStruct((batch * seq, w), BF16),
                   jax.ShapeDtypeStruct((batch, 1, w), F32),
                   jax.ShapeDtypeStruct((batch, CONV_W - 1, w), F32)],
        scratch_shapes=[pltpu.VMEM((SUBLANES + tc, wt), F32), pltpu.VMEM((tc, wt), F32),
                        pltpu.VMEM((tc, wt), F32), pltpu.VMEM((1, wt), F32)],
        compiler_params=_params(("arbitrary", "arbitrary", "arbitrary")),
        name="lru_prompt",
    )(proj, proj, p["conv_w"], p["conv_b"], p["w_a"], p["w_x"], p["b_a"], p["b_x"], p["lam"])


def _lru_sample_kernel(gate_ref, u_ref, h0_ref, c0_ref, cw_ref, cb_ref, wa_ref, wx_ref, ba_ref, bx_ref, lam_ref,
                       o_ref, h_ref, ct_ref):
    t_new = u_ref.shape[0]
    tail = CONV_W - 1
    upad = [c0_ref[t] for t in range(tail)] + [u_ref[t] for t in range(t_new)]
    cw = cw_ref[...]
    h = h0_ref[...]
    for t in range(t_new):
        taps = 0.0
        for k in range(CONV_W):
            taps = taps + upad[t + k] * cw[k:k + 1, :]
        conv = cb_ref[...] + taps
        a, b = _lru_gates(conv, wa_ref, wx_ref, ba_ref[...], bx_ref[...], lam_ref[...])
        h = a * h + b
        o_ref[t] = (h * jax.nn.gelu(gate_ref[t])).astype(o_ref.dtype)
    h_ref[...] = h
    for t in range(tail):
        ct_ref[t] = upad[t_new + t]


def lru_sample(proj_t, h0, conv0_t, p):
    t_new, b, w2 = proj_t.shape
    w = w2 // 2
    wt = LRU_WT
    nw = w // wt
    tail = CONV_W - 1
    return pl.pallas_call(
        _lru_sample_kernel,
        grid=(nw,),
        in_specs=[pl.BlockSpec((t_new, b, wt), lambda i: (0, 0, i)),
                  pl.BlockSpec((t_new, b, wt), lambda i: (0, 0, nw + i)),
                  pl.BlockSpec((b, wt), lambda i: (0, i)),
                  pl.BlockSpec((tail, b, wt), lambda i: (0, 0, i))] + _lru_specs(wt),
        out_specs=[pl.BlockSpec((t_new, b, wt), lambda i: (0, 0, i)),
                   pl.BlockSpec((b, wt), lambda i: (0, i)),
                   pl.BlockSpec((tail, b, wt), lambda i: (0, 0, i))],
        out_shape=[jax.ShapeDtypeStruct((t_new, b, w), BF16),
                   jax.ShapeDtypeStruct((b, w), F32),
                   jax.ShapeDtypeStruct((tail, b, w), F32)],
        compiler_params=_params(("arbitrary",)),
        name="lru_sample",
    )(proj_t, proj_t, h0, conv0_t, p["conv_w"], p["conv_b"], p["w_a"], p["w_x"], p["b_a"], p["b_x"], p["lam"])


def _gather_rows_kernel(tok_ref, nvalid_ref, x_hbm, o_hbm, sem):
    i = pl.program_id(0)
    rows = MOE_ROWS

    def copy(r):
        t = tok_ref[i * rows + r]
        return pltpu.make_async_copy(x_hbm.at[pl.ds(t, 1)], o_hbm.at[pl.ds(i * rows + r, 1)], sem)

    @pl.when(i < nvalid_ref[0])
    def _():
        def issue(r, c):
            copy(r).start()
            return c

        def drain(r, c):
            copy(r).wait()
            return c

        lax.fori_loop(0, rows, issue, 0)
        lax.fori_loop(0, rows, drain, 0)


def gather_rows(x, slot_tok, nvalid, nblk):
    d = x.shape[1]
    return pl.pallas_call(
        _gather_rows_kernel,
        grid_spec=pltpu.PrefetchScalarGridSpec(
            num_scalar_prefetch=2,
            grid=(nblk,),
            in_specs=[pl.BlockSpec(memory_space=pl.ANY)],
            out_specs=pl.BlockSpec(memory_space=pl.ANY),
            scratch_shapes=[pltpu.SemaphoreType.DMA(())]),
        out_shape=jax.ShapeDtypeStruct((nblk * MOE_ROWS, d), x.dtype),
        compiler_params=_params(("arbitrary",)),
        name="moe_gather",
    )(slot_tok, nvalid, x)


def _expert_changed(blk_e_ref, i):
    prev = blk_e_ref[jnp.maximum(i - 1, 0)]
    return (i == 0) | (blk_e_ref[i] != prev)


def _ffn_up_kernel(blk_e_ref, nvalid_ref, x_ref, w1_ref, w3_ref, o_ref, w1b, w3b):
    i = pl.program_id(0)
    valid = i < nvalid_ref[0]

    @pl.when(valid & _expert_changed(blk_e_ref, i))
    def _():
        w1b[...] = w1_ref[0].astype(BF16)
        w3b[...] = w3_ref[0].astype(BF16)

    @pl.when(valid)
    def _():
        x = x_ref[...].astype(BF16)
        a = jnp.dot(x, w1b[...], preferred_element_type=F32)
        b = jnp.dot(x, w3b[...], preferred_element_type=F32)
        o_ref[...] = (jax.nn.silu(a) * b).astype(o_ref.dtype)


def _ffn_down_kernel(blk_e_ref, nvalid_ref, h_ref, w2_ref, o_ref, w2b):
    i = pl.program_id(0)
    valid = i < nvalid_ref[0]

    @pl.when(valid & _expert_changed(blk_e_ref, i))
    def _():
        w2b[...] = w2_ref[0].astype(BF16)

    @pl.when(valid)
    def _():
        o_ref[...] = jnp.dot(h_ref[...], w2b[...], preferred_element_type=F32)


def _row_block(i, blk_e, nvalid):
    return (jnp.minimum(i, nvalid[0] - 1), 0)


def expert_ffn(xs, blk_e, nvalid, w1, w3, w2, nblk):
    d = xs.shape[1]
    f = w1.shape[2]
    rows = MOE_ROWS
    wspec_up = pl.BlockSpec((1, d, f), lambda i, blk_e, nvalid: (blk_e[i], 0, 0))
    hdn = pl.pallas_call(
        _ffn_up_kernel,
        grid_spec=pltpu.PrefetchScalarGridSpec(
            num_scalar_prefetch=2,
            grid=(nblk,),
            in_specs=[pl.BlockSpec((rows, d), _row_block), wspec_up, wspec_up],
            out_specs=pl.BlockSpec((rows, f), _row_block),
            scratch_shapes=[pltpu.VMEM((d, f), BF16), pltpu.VMEM((d, f), BF16)]),
        out_shape=jax.ShapeDtypeStruct((nblk * rows, f), BF16),
        compiler_params=_params(("arbitrary",), 60 * 1024 * 1024),
        name="moe_ffn_up",
    )(blk_e, nvalid, xs, w1, w3)
    return pl.pallas_call(
        _ffn_down_kernel,
        grid_spec=pltpu.PrefetchScalarGridSpec(
            num_scalar_prefetch=2,
            grid=(nblk,),
            in_specs=[pl.BlockSpec((rows, f), _row_block),
                      pl.BlockSpec((1, f, d), lambda i, blk_e, nvalid: (blk_e[i], 0, 0))],
            out_specs=pl.BlockSpec((rows, d), _row_block),
            scratch_shapes=[pltpu.VMEM((f, d), BF16)]),
        out_shape=jax.ShapeDtypeStruct((nblk * rows, d), F32),
        compiler_params=_params(("arbitrary",)),
        name="moe_ffn_down",
    )(blk_e, nvalid, hdn, w2)


def _combine_kernel(dest_ref, x_ref, r_ref, g_ref, y_hbm, x_out_ref, n_out_ref, ybuf, sem, *, top_k):
    i = pl.program_id(0)
    n = pl.num_programs(0)
    rows = x_ref.shape[0]

    def copy(step, slot, r, k):
        src = dest_ref[(step * rows + r) * top_k + k]
        return pltpu.make_async_copy(y_hbm.at[pl.ds(src, 1)], ybuf.at[slot, k, pl.ds(r, 1)], sem.at[slot])

    def start_step(step, slot):
        def issue(r, c):
            for k in range(top_k):
                copy(step, slot, r, k).start()
            return c
        lax.fori_loop(0, rows, issue, 0)

    slot = i % 2

    @pl.when(i == 0)
    def _():
        start_step(0, 0)

    @pl.when(i + 1 < n)
    def _():
        start_step(i + 1, 1 - slot)

    def drain(r, c):
        for k in range(top_k):
            copy(i, slot, r, k).wait()
        return c
    lax.fori_loop(0, rows, drain, 0)

    rec = r_ref[...]
    moe = ybuf[slot, 0] * rec[:, 0:1]
    for k in range(1, top_k):
        moe = moe + ybuf[slot, k] * rec[:, k:k + 1]
    x_new = x_ref[...] + moe
    x_out_ref[...] = x_new
    n_out_ref[...] = _rms(x_new, g_ref[...]).astype(n_out_ref.dtype)


def moe_combine(x, rec, dest, y, g_next, norm_dtype):
    m, d = x.shape
    rows = MOE_ROWS
    top_k = 2
    return pl.pallas_call(
        functools.partial(_combine_kernel, top_k=top_k),
        grid_spec=pltpu.PrefetchScalarGridSpec(
            num_scalar_prefetch=1,
            grid=(m // rows,),
            in_specs=[pl.BlockSpec((rows, d), lambda i, dest: (i, 0)),
                      pl.BlockSpec((rows, ROUTE_LANES), lambda i, dest: (i, 0)),
                      pl.BlockSpec((1, d), lambda i, dest: (0, 0)),
                      pl.BlockSpec(memory_space=pl.ANY)],
            out_specs=[pl.BlockSpec((rows, d), lambda i, dest: (i, 0)),
                       pl.BlockSpec((rows, d), lambda i, dest: (i, 0))],
            scratch_shapes=[pltpu.VMEM((2, top_k, rows, d), F32),
                            pltpu.SemaphoreType.DMA((2,))]),
        out_shape=[jax.ShapeDtypeStruct((m, d), F32),
                   jax.ShapeDtypeStruct((m, d), norm_dtype)],
        compiler_params=_params(("arbitrary",)),
        name="moe_combine",
    )(dest, x, rec, g_next.reshape(1, d), y)


def _dispatch(expert, nblk):
    m, top_k = expert.shape
    a = m * top_k
    rows = MOE_ROWS
    flat_e = expert.reshape(a)
    order = jnp.argsort(flat_e, stable=True).astype(jnp.int32)
    counts = jnp.sum((flat_e[:, None] == jnp.arange(N_EXPERTS, dtype=jnp.int32)[None, :]).astype(jnp.int32), axis=0)
    pcounts = (counts + rows - 1) // rows * rows
    start = jnp.cumsum(counts) - counts
    pend = jnp.cumsum(pcounts)
    pstart = pend - pcounts
    nvalid = (pend[-1] // rows).astype(jnp.int32)
    se = flat_e[order]
    dest_sorted = pstart[se] + (jnp.arange(a, dtype=jnp.int32) - start[se])
    dest = jnp.zeros((a,), jnp.int32).at[order].set(dest_sorted.astype(jnp.int32))
    blk = jnp.arange(nblk, dtype=jnp.int32)
    blk_e = jnp.minimum(jnp.searchsorted(pend, jnp.minimum(blk, nvalid - 1) * rows, side="right"),
                        N_EXPERTS - 1).astype(jnp.int32)
    slot = jnp.arange(nblk * rows, dtype=jnp.int32)
    slot_e = jnp.repeat(blk_e, rows)
    q = slot - pstart[slot_e]
    real = q < counts[slot_e]
    src_sorted = jnp.clip(start[slot_e] + q, 0, a - 1)
    slot_tok = jnp.where(real, order[src_sorted] // top_k, 0).astype(jnp.int32)
    return slot_tok, dest, blk_e, nvalid.reshape(1)


def moe_layer(x, g_ffn, w_rg, b_rg, w_re, b_re, w1, w3, w2, g_next, norm_dtype):
    m, d = x.shape
    pad = ROUTE_LANES - N_GROUPS - N_EXPERTS
    w_router = jnp.concatenate([w_rg, w_re, jnp.zeros((d, pad), F32)], axis=1)
    b_router = jnp.concatenate([b_rg, b_re, jnp.zeros((pad,), F32)]).reshape(1, ROUTE_LANES)
    xn, rec = norm_router(x, g_ffn, w_router, b_router)
    expert = rec[:, 2:4].astype(jnp.int32)
    nblk = -(-(m * 2) // MOE_ROWS) + N_EXPERTS
    slot_tok, dest, blk_e, nvalid = _dispatch(expert, nblk)
    xs = gather_rows(xn, slot_tok, nvalid, nblk)
    y = expert_ffn(xs, blk_e, nvalid, w1, w3, w2, nblk)
    return moe_combine(x, rec, dest, y, g_next, norm_dtype)


def kernel(x_prompt, x_sample, cache_k, cache_v, state_h, state_conv, norm_mix, norm_ffn, norm_final, attn_w_qkv, attn_w_o, attn_sinks, lru_w_in, lru_conv_w, lru_conv_b, lru_w_a, lru_b_a, lru_w_x, lru_b_x, lru_lambda, lru_w_out, moe_w_router_group, moe_b_router_group, moe_w_router_expert, moe_b_router_expert, moe_w1, moe_w3, moe_w2):
    bp, seq, d = x_prompt.shape
    bs, t_new, _ = x_sample.shape
    mp, ms = bp * seq, bs * t_new
    depth = norm_mix.shape[0]
    cache = cache_k.shape[2]
    kvw = N_KV_HEADS * HEAD_DIM
    qw = N_HEADS * HEAD_DIM

    x = jnp.concatenate([x_prompt.reshape(mp, d), x_sample.reshape(ms, d)], axis=0)
    xn = rmsnorm(x, norm_mix[0], BF16)

    kp_l, vp_l, hp_l, cp_l = [], [], [], []
    ks_l, vs_l, hs_l, cs_l = [], [], [], []
    for i in range(depth):
        j = i // 2
        if i % 2 == 0:
            qkv = matmul(xn, attn_w_qkv[j])
            o_p, kp, vp = attn_prompt(qkv, attn_sinks[j], bp, seq)
            qkv_s = qkv[mp:].reshape(bs, t_new, qw + 2 * kvw)
            q_arr = (qkv_s[:, :, :qw].reshape(bs, t_new, N_KV_HEADS, GQA, HEAD_DIM)
                     .transpose(0, 2, 3, 1, 4).reshape(bs, N_KV_HEADS, GQA * t_new, HEAD_DIM))
            o_s, kn, vn = attn_sample(q_arr, qkv_s[:, :, qw:qw + kvw], qkv_s[:, :, qw + kvw:],
                                      cache_k[j].reshape(bs, cache, kvw), cache_v[j].reshape(bs, cache, kvw),
                                      attn_sinks[j])
            o_s = (o_s.reshape(bs, N_KV_HEADS, GQA, t_new, HEAD_DIM).transpose(0, 3, 1, 2, 4)
                   .reshape(ms, qw).astype(BF16))
            mixed = jnp.concatenate([o_p, o_s], axis=0)
            x = matmul(mixed, attn_w_o[j], res=x)
            kp_l.append(kp.reshape(bp, WINDOW, N_KV_HEADS, HEAD_DIM))
            vp_l.append(vp.reshape(bp, WINDOW, N_KV_HEADS, HEAD_DIM))
            ks_l.append(kn.reshape(bs, cache, N_KV_HEADS, HEAD_DIM))
            vs_l.append(vn.reshape(bs, cache, N_KV_HEADS, HEAD_DIM))
        else:
            w = lru_w_in.shape[2] // 2
            p = dict(conv_w=lru_conv_w[j], conv_b=lru_conv_b[j].reshape(1, w), w_a=lru_w_a[j], w_x=lru_w_x[j],
                     b_a=lru_b_a[j].reshape(1, w), b_x=lru_b_x[j].reshape(1, w), lam=lru_lambda[j].reshape(1, w))
            proj = matmul(xn, lru_w_in[j])
            hg_p, hp, cp = lru_prompt(proj, bp, seq, p)
            proj_t = proj[mp:].reshape(bs, t_new, 2 * w).transpose(1, 0, 2)
            hg_s, hn, cn = lru_sample(proj_t, state_h[j], state_conv[j].transpose(1, 0, 2), p)
            mixed = jnp.concatenate([hg_p, hg_s.transpose(1, 0, 2).reshape(ms, w)], axis=0)
            x = matmul(mixed, lru_w_out[j], res=x)
            hp_l.append(hp.reshape(bp, w))
            cp_l.append(cp)
            hs_l.append(hn)
            cs_l.append(cn.transpose(1, 0, 2))
        last = i == depth - 1
        g_next = norm_final if last else norm_mix[i + 1]
        x, xn = moe_layer(x, norm_ffn[i], moe_w_router_group[i], moe_b_router_group[i],
                          moe_w_router_expert[i], moe_b_router_expert[i],
                          moe_w1[i], moe_w3[i], moe_w2[i], g_next, F32 if last else BF16)

    y_prompt = xn[:mp].reshape(bp, seq, d)
    y_sample = xn[mp:].reshape(bs, t_new, d)
    return (y_prompt, y_sample,
            jnp.stack(kp_l), jnp.stack(vp_l), jnp.stack(hp_l), jnp.stack(cp_l),
            jnp.stack(ks_l), jnp.stack(vs_l), jnp.stack(hs_l), jnp.stack(cs_l))
```
